```python
import math
import jax, jax.numpy as jnp
from jax import lax
import numpy as np


D_MODEL = 1024
BATCH = 1
SEQ = 16384
DEPTH = 4
DEC_BATCH = 8
DEC_SEQ = 8192
PAST_LEN = 128

N_MIXERS = 2
HEAD_DIM = 128
N_Q_HEADS = D_MODEL // HEAD_DIM
N_KV_HEADS = 2
Q_PER_KV = N_Q_HEADS // N_KV_HEADS
QKV_DIM = (N_Q_HEADS + 2 * N_KV_HEADS) * HEAD_DIM
ROPE_THETA = 10000.0
Q_BLOCK = 128
GRID_W = 64
SSM_WIDTH = D_MODEL
GROUP_CH = 16
N_GROUPS = SSM_WIDTH // GROUP_CH
STATE_DIM = 64
DT_MIN = 0.001
DT_MAX = 0.1
D_FF = -(-8 * D_MODEL // (3 * 256)) * 256
N_ATTN_LAYERS = (DEPTH + 1) // 2
N_SSM_LAYERS = DEPTH // 2
EPS = 1e-6

kernel_name = 'bidir_hybrid_attn_s5_encoder'


def rmsnorm(x, gain):
    x32 = x.astype(jnp.float32)
    y = x32 * lax.rsqrt(jnp.mean(x32 * x32, axis=-1, keepdims=True) + EPS)
    return (y * gain.astype(jnp.float32)).astype(x.dtype)


def modulate(h, shift, scale):
    return h * (1.0 + scale[:, None, :]) + shift[:, None, :]


def axial_rope_angles(L):
    rows = L // GRID_W
    row = jnp.repeat(jnp.arange(rows), GRID_W).astype(jnp.float32)
    col = jnp.tile(jnp.arange(GRID_W), rows).astype(jnp.float32)
    n_freq = HEAD_DIM // 4
    inv = ROPE_THETA ** (-jnp.arange(n_freq, dtype=jnp.float32) / n_freq)
    return row[:, None] * inv, col[:, None] * inv


def rope_half(x, ang):
    x1, x2 = jnp.split(x, 2, axis=-1)
    cos = jnp.cos(ang)[None, :, None, :]
    sin = jnp.sin(ang)[None, :, None, :]
    return jnp.concatenate([x1 * cos - x2 * sin, x2 * cos + x1 * sin], axis=-1)


def apply_axial_rope(x, ang_r, ang_c):
    x32 = x.astype(jnp.float32)
    half = HEAD_DIM // 2
    out = jnp.concatenate([rope_half(x32[..., :half], ang_r), rope_half(x32[..., half:], ang_c)], axis=-1)
    return out.astype(x.dtype)


def attention_mixer(h, w_qkv, q_gain, k_gain, w_o, ang_r, ang_c):
    B, L, _ = h.shape
    qkv = h @ w_qkv
    q, k, v = jnp.split(qkv, [N_Q_HEADS * HEAD_DIM, (N_Q_HEADS + N_KV_HEADS) * HEAD_DIM], axis=-1)
    q = q.reshape(B, L, N_Q_HEADS, HEAD_DIM)
    k = k.reshape(B, L, N_KV_HEADS, HEAD_DIM)
    v = v.reshape(B, L, N_KV_HEADS, HEAD_DIM)
    q = apply_axial_rope(rmsnorm(q, q_gain), ang_r, ang_c)
    k = apply_axial_rope(rmsnorm(k, k_gain), ang_r, ang_c)
    nb = L // Q_BLOCK
    qb = q.reshape(B, nb, Q_BLOCK, N_KV_HEADS, Q_PER_KV, HEAD_DIM).transpose(1, 0, 2, 3, 4, 5)
    scale = HEAD_DIM ** -0.5

    def attend(q_blk):
        s = jnp.einsum('bqkgd,bskd->bkgqs', q_blk, k, preferred_element_type=jnp.float32) * scale
        p = jax.nn.softmax(s, axis=-1)
        return jnp.einsum('bkgqs,bskd->bqkgd', p.astype(v.dtype), v)

    o = lax.map(attend, qb)
    o = o.transpose(1, 0, 2, 3, 4, 5).reshape(B, L, N_Q_HEADS * HEAD_DIM)
    return o @ w_o


def complex_linear_combine(e1, e2):
    a1r, a1i, b1r, b1i = e1
    a2r, a2i, b2r, b2i = e2
    return (a2r * a1r - a2i * a1i,
            a2r * a1i + a2i * a1r,
            a2r * b1r - a2i * b1i + b2r,
            a2r * b1i + a2i * b1r + b2i)


def s5_direction(ug, a_re, a_im, log_dt, b_re, b_im, c_re, c_im, reverse):
    dt = jnp.exp(log_dt)[:, None]
    mag = jnp.exp(dt * a_re)
    ab_re = mag * jnp.cos(dt * a_im)
    ab_im = mag * jnp.sin(dt * a_im)
    den = a_re * a_re + a_im * a_im
    nr = ab_re - 1.0
    f_re = (nr * a_re + ab_im * a_im) / den
    f_im = (ab_im * a_re - nr * a_im) / den
    bb_re = f_re[..., None] * b_re - f_im[..., None] * b_im
    bb_im = f_re[..., None] * b_im + f_im[..., None] * b_re
    bu_re = jnp.einsum('lgh,gph->lgp', ug, bb_re)
    bu_im = jnp.einsum('lgh,gph->lgp', ug, bb_im)
    shape = bu_re.shape
    _, _, s_re, s_im = lax.associative_scan(
        complex_linear_combine,
        (jnp.broadcast_to(ab_re, shape), jnp.broadcast_to(ab_im, shape), bu_re, bu_im),
        axis=0, reverse=reverse)
    return jnp.einsum('ghp,lgp->lgh', c_re, s_re) - jnp.einsum('ghp,lgp->lgh', c_im, s_im)


def ssm_mixer(h, w_in, a_re, a_im, log_dt, b_re, b_im, c_re, c_im, d, w_glu):
    B, L, _ = h.shape
    f32 = jnp.float32
    u = (h @ w_in).astype(f32)

    def one_seq(u_seq):
        ug = u_seq.reshape(L, N_GROUPS, GROUP_CH)
        y = d.astype(f32) * u_seq
        for direction, reverse in ((0, False), (1, True)):
            y = y + s5_direction(ug, a_re[direction].astype(f32), a_im[direction].astype(f32),
                                 log_dt[direction].astype(f32), b_re[direction].astype(f32),
                                 b_im[direction].astype(f32), c_re[direction].astype(f32),
                                 c_im[direction].astype(f32), reverse).reshape(L, SSM_WIDTH)
        return y

    y = lax.map(one_seq, u)
    y = jax.nn.gelu(y).astype(h.dtype)
    val, gate = jnp.split(y @ w_glu, 2, axis=-1)
    return val * jax.nn.sigmoid(gate)


def swiglu(h, w_gu, w_down):
    g, up = jnp.split(h @ w_gu, 2, axis=-1)
    return (jax.nn.silu(g) * up) @ w_down


def trunk(x, c, norm_gain, w_mod, b_mod, attn_w_qkv, attn_q_gain, attn_k_gain, attn_w_o,
          ssm_w_in, ssm_a_re, ssm_a_im, ssm_log_dt, ssm_b_re, ssm_b_im, ssm_c_re, ssm_c_im,
          ssm_d, ssm_w_glu, ffn_w_gu, ffn_w_down, final_gain):
    L = x.shape[1]
    ang_r, ang_c = axial_rope_angles(L)
    c_act = jax.nn.silu(c)
    for i in range(DEPTH):
        mod = c_act @ w_mod[i] + b_mod[i]
        sh1, sc1, g1, sh2, sc2, g2 = jnp.split(mod, 6, axis=-1)
        h = modulate(rmsnorm(x, norm_gain[i, 0]), sh1, sc1)
        j = i // N_MIXERS
        if i % N_MIXERS == 0:
            m = attention_mixer(h, attn_w_qkv[j], attn_q_gain[j], attn_k_gain[j], attn_w_o[j], ang_r, ang_c)
        else:
            m = ssm_mixer(h, ssm_w_in[j], ssm_a_re[j], ssm_a_im[j], ssm_log_dt[j], ssm_b_re[j],
                          ssm_b_im[j], ssm_c_re[j], ssm_c_im[j], ssm_d[j], ssm_w_glu[j])
        x = x + g1[:, None, :] * m
        h = modulate(rmsnorm(x, norm_gain[i, 1]), sh2, sc2)
        x = x + g2[:, None, :] * swiglu(h, ffn_w_gu[i], ffn_w_down[i])
    return rmsnorm(x, final_gain)


def setup_inputs(seed: int = 0) -> dict:
    key = jax.random.key(seed)
    ks = jax.random.split(key, 24)
    f32 = jnp.float32

    def nrm(k, shape, scale):
        return jax.random.normal(k, shape, f32) * scale

    NA, NS = N_ATTN_LAYERS, N_SSM_LAYERS
    n_idx = jnp.arange(STATE_DIM, dtype=f32)
    return {
        'x_prompt': nrm(ks[0], (BATCH, SEQ, D_MODEL), 1.0),
        'x_sample': nrm(ks[1], (DEC_BATCH, DEC_SEQ, D_MODEL), 1.0),
        'c_prompt': nrm(ks[2], (BATCH, D_MODEL), 1.0),
        'c_sample': nrm(ks[3], (DEC_BATCH, D_MODEL), 1.0),
        'norm_gain': 1.0 + nrm(ks[4], (DEPTH, 2, D_MODEL), 0.02),
        'w_mod': nrm(ks[5], (DEPTH, D_MODEL, 6 * D_MODEL), 0.5 * D_MODEL ** -0.5),
        'b_mod': nrm(ks[6], (DEPTH, 6 * D_MODEL), 0.02),
        'attn_w_qkv': nrm(ks[7], (NA, D_MODEL, QKV_DIM), D_MODEL ** -0.5),
        'attn_q_gain': 1.0 + nrm(ks[8], (NA, HEAD_DIM), 0.02),
        'attn_k_gain': 1.0 + nrm(ks[9], (NA, HEAD_DIM), 0.02),
        'attn_w_o': nrm(ks[10], (NA, N_Q_HEADS * HEAD_DIM, D_MODEL), (N_Q_HEADS * HEAD_DIM) ** -0.5),
        'ssm_w_in': nrm(ks[11], (NS, D_MODEL, SSM_WIDTH), D_MODEL ** -0.5),
        'ssm_a_re': -0.5 + nrm(ks[12], (NS, 2, N_GROUPS, STATE_DIM), 0.01),
        'ssm_a_im': math.pi * n_idx + nrm(ks[13], (NS, 2, N_GROUPS, STATE_DIM), 0.01),
        'ssm_log_dt': jax.random.uniform(ks[14], (NS, 2, N_GROUPS), f32, math.log(DT_MIN), math.log(DT_MAX)),
        'ssm_b_re': nrm(ks[15], (NS, 2, N_GROUPS, STATE_DIM, GROUP_CH), (2 * GROUP_CH) ** -0.5),
        'ssm_b_im': nrm(ks[16], (NS, 2, N_GROUPS, STATE_DIM, GROUP_CH), (2 * GROUP_CH) ** -0.5),
        'ssm_c_re': nrm(ks[17], (NS, 2, N_GROUPS, GROUP_CH, STATE_DIM), STATE_DIM ** -0.5),
        'ssm_c_im': nrm(ks[18], (NS, 2, N_GROUPS, GROUP_CH, STATE_DIM), STATE_DIM ** -0.5),
        'ssm_d': nrm(ks[19], (NS, SSM_WIDTH), 1.0),
        'ssm_w_glu': nrm(ks[20], (NS, SSM_WIDTH, 2 * D_MODEL), SSM_WIDTH ** -0.5),
        'ffn_w_gu': nrm(ks[21], (DEPTH, D_MODEL, 2 * D_FF), D_MODEL ** -0.5),
        'ffn_w_down': nrm(ks[22], (DEPTH, D_FF, D_MODEL), D_FF ** -0.5),
        'final_gain': 1.0 + nrm(ks[23], (D_MODEL,), 0.02),
    }


def reference(x_prompt, x_sample, c_prompt, c_sample, norm_gain, w_mod, b_mod, attn_w_qkv,
              attn_q_gain, attn_k_gain, attn_w_o, ssm_w_in, ssm_a_re, ssm_a_im, ssm_log_dt,
              ssm_b_re, ssm_b_im, ssm_c_re, ssm_c_im, ssm_d, ssm_w_glu, ffn_w_gu, ffn_w_down,
              final_gain):
    def run(x, c):
        return trunk(x, c, norm_gain=norm_gain, w_mod=w_mod, b_mod=b_mod, attn_w_qkv=attn_w_qkv,
                     attn_q_gain=attn_q_gain, attn_k_gain=attn_k_gain, attn_w_o=attn_w_o,
                     ssm_w_in=ssm_w_in, ssm_a_re=ssm_a_re, ssm_a_im=ssm_a_im, ssm_log_dt=ssm_log_dt,
                     ssm_b_re=ssm_b_re, ssm_b_im=ssm_b_im, ssm_c_re=ssm_c_re, ssm_c_im=ssm_c_im,
                     ssm_d=ssm_d, ssm_w_glu=ssm_w_glu, ffn_w_gu=ffn_w_gu, ffn_w_down=ffn_w_down,
                     final_gain=final_gain)

    y_prompt = run(x_prompt, c_prompt)
    y_sample = run(x_sample, c_sample)
    return (y_prompt, y_sample)
```

```python
import functools
import math

import numpy as np
import jax
import jax.numpy as jnp
from jax import lax
from jax.experimental import pallas as pl
from jax.experimental.pallas import tpu as pltpu

F32 = jnp.float32
BF16 = jnp.bfloat16
HIGHEST = lax.Precision.HIGHEST

HEAD_DIM = 128
GRID_W = 64
ROPE_THETA = 10000.0
EPS = 1e-6
LANES = 128
CHUNK = LANES
GROUP_CH = 16
STATE_DIM = 64
VMEM_LIMIT = 56 * 1024 * 1024

_NT = (((1,), (1,)), ((), ()))
_TN = (((0,), (0,)), ((), ()))


def _params(n_axes, vmem=VMEM_LIMIT):
    return pltpu.CompilerParams(dimension_semantics=("parallel",) * n_axes,
                                vmem_limit_bytes=vmem)


def _rms(x, gain):
    ms = jnp.mean(x * x, axis=-1, keepdims=True)
    return x * lax.rsqrt(ms + EPS) * gain


def _rms_mod(x, gain, scale, shift):
    return _rms(x, gain) * (1.0 + scale) + shift


def _mod_kernel(c_ref, w_ref, b_ref, o_ref):
    c = c_ref[...]
    ca = c * jax.nn.sigmoid(c)
    o_ref[0] = jnp.dot(ca, w_ref[0], preferred_element_type=F32, precision=HIGHEST) + b_ref[0]


def _mod_call(c_pad, w_mod, b_mod):
    depth, d, d6 = w_mod.shape
    rows = c_pad.shape[0]
    return pl.pallas_call(
        _mod_kernel,
        grid=(depth, d6 // d),
        in_specs=[pl.BlockSpec((rows, d), lambda i, j: (0, 0)),
                  pl.BlockSpec((1, d, d), lambda i, j: (i, 0, j)),
                  pl.BlockSpec((1, 1, d), lambda i, j: (i, 0, j))],
        out_specs=pl.BlockSpec((1, rows, d), lambda i, j: (i, 0, j)),
        out_shape=jax.ShapeDtypeStruct((depth, rows, d6), F32),
        compiler_params=_params(2),
        name="adaln_mod",
    )(c_pad, w_mod, b_mod.reshape(depth, 1, d6))


def _attn_in_kernel(x_ref, mod_ref, ng_ref, w_ref, qg_ref, kg_ref, cos_ref, sa_ref, sb_ref,
                    q_ref, k_ref, v_ref, *, n_q, n_kv):
    mod = mod_ref[0]
    h = _rms_mod(x_ref[...], ng_ref[...], mod[1:2], mod[0:1])
    qkv = jnp.dot(h.astype(BF16), w_ref[...], preferred_element_type=F32)
    cos = cos_ref[...]
    sa = sa_ref[...]
    sb = sb_ref[...]
    quarter = HEAD_DIM // 4

    def norm_rope(t, gain):
        t = _rms(t, gain)
        return t * cos + pltpu.roll(t, HEAD_DIM - quarter, 1) * sa + pltpu.roll(t, quarter, 1) * sb

    scale = HEAD_DIM ** -0.5
    for hd in range(n_q):
        sl = slice(hd * HEAD_DIM, (hd + 1) * HEAD_DIM)
        q_ref[:, sl] = (norm_rope(qkv[:, sl], qg_ref[...]) * scale).astype(BF16)
    for hd in range(n_kv):
        src = slice((n_q + hd) * HEAD_DIM, (n_q + hd + 1) * HEAD_DIM)
        k_ref[:, hd * HEAD_DIM:(hd + 1) * HEAD_DIM] = norm_rope(qkv[:, src], kg_ref[...]).astype(BF16)
    v_ref[...] = qkv[:, (n_q + n_kv) * HEAD_DIM:].astype(BF16)


def _attn_in_call(x, modv, ng, w_qkv, qg, kg, rope, *, seq_len, tm):
    n, d = x.shape
    n_q = d // HEAD_DIM
    n_kv = (w_qkv.shape[1] // HEAD_DIM - n_q) // 2
    per_seq = seq_len // tm
    tok = lambda i: (i, 0)
    const = lambda i: (0, 0)
    pos = lambda i: (i % per_seq, 0)
    kv_w = n_kv * HEAD_DIM
    return pl.pallas_call(
        functools.partial(_attn_in_kernel, n_q=n_q, n_kv=n_kv),
        grid=(n // tm,),
        in_specs=[pl.BlockSpec((tm, d), tok),
                  pl.BlockSpec((1, 8, d), lambda i: (i // per_seq, 0, 0)),
                  pl.BlockSpec((1, d), const),
                  pl.BlockSpec(w_qkv.shape, const),
                  pl.BlockSpec((1, HEAD_DIM), const),
                  pl.BlockSpec((1, HEAD_DIM), const),
                  pl.BlockSpec((tm, HEAD_DIM), pos),
                  pl.BlockSpec((tm, HEAD_DIM), pos),
                  pl.BlockSpec((tm, HEAD_DIM), pos)],
        out_specs=[pl.BlockSpec((tm, d), tok),
                   pl.BlockSpec((tm, kv_w), tok),
                   pl.BlockSpec((tm, kv_w), tok)],
        out_shape=[jax.ShapeDtypeStruct((n, d), BF16),
                   jax.ShapeDtypeStruct((n, kv_w), BF16),
                   jax.ShapeDtypeStruct((n, kv_w), BF16)],
        compiler_params=_params(1),
        name="attn_in",
    )(x, modv, ng, w_qkv, qg, kg, *rope)


def _flash_kernel(q_ref, k_ref, v_ref, o_ref, *, tk, group):
    tq = q_ref.shape[0]
    seq = k_ref.shape[0]
    q = jnp.concatenate([q_ref[:, g * HEAD_DIM:(g + 1) * HEAD_DIM] for g in range(group)], axis=0)
    rows = group * tq

    def body(j, carry):
        m, l, acc = carry
        off = pl.multiple_of(j * tk, tk)
        kc = k_ref[pl.ds(off, tk), :]
        vc = v_ref[pl.ds(off, tk), :]
        s = lax.dot_general(q, kc, _NT, preferred_element_type=F32)
        m_new = jnp.maximum(m, jnp.max(s, axis=-1, keepdims=True))
        alpha = jnp.exp(m - m_new)
        p = jnp.exp(s - m_new)
        l = alpha * l + jnp.sum(p, axis=-1, keepdims=True)
        acc = alpha * acc + jnp.dot(p.astype(BF16), vc, preferred_element_type=F32)
        return m_new, l, acc

    init = (jnp.full((rows, 1), -jnp.inf, F32), jnp.zeros((rows, 1), F32),
            jnp.zeros((rows, HEAD_DIM), F32))
    _, l, acc = lax.fori_loop(0, seq // tk, body, init)
    o = acc / l
    for g in range(group):
        o_ref[:, g * HEAD_DIM:(g + 1) * HEAD_DIM] = o[g * tq:(g + 1) * tq].astype(BF16)


def _flash_call(q, k, v, *, batch, seq_len, tq, tk):
    n, d = q.shape
    n_kv = k.shape[1] // HEAD_DIM
    group = d // HEAD_DIM // n_kv
    gw = group * HEAD_DIM
    per_seq = seq_len // tq
    return pl.pallas_call(
        functools.partial(_flash_kernel, tk=tk, group=group),
        grid=(batch, n_kv, per_seq),
        in_specs=[pl.BlockSpec((tq, gw), lambda b, h, i: (b * per_seq + i, h)),
                  pl.BlockSpec((seq_len, HEAD_DIM), lambda b, h, i: (b, h)),
                  pl.BlockSpec((seq_len, HEAD_DIM), lambda b, h, i: (b, h))],
        out_specs=pl.BlockSpec((tq, gw), lambda b, h, i: (b * per_seq + i, h)),
        out_shape=jax.ShapeDtypeStruct((n, d), BF16),
        compiler_params=_params(3),
        name="flash_attn",
    )(q, k, v)


def _ffn_tail(x1, mod, ng2, wgu_ref, wd_ref, fg_ref, n_ff_chunks, final):
    d_ff = wd_ref.shape[0]
    tf = d_ff // n_ff_chunks
    h2 = _rms_mod(x1, ng2, mod[4:5], mod[3:4]).astype(BF16)
    acc = jnp.zeros_like(x1)
    for c in range(n_ff_chunks):
        g = jnp.dot(h2, wgu_ref[:, c * tf:(c + 1) * tf], preferred_element_type=F32)
        u = jnp.dot(h2, wgu_ref[:, d_ff + c * tf:d_ff + (c + 1) * tf], preferred_element_type=F32)
        a = (g * jax.nn.sigmoid(g) * u).astype(BF16)
        acc = acc + jnp.dot(a, wd_ref[c * tf:(c + 1) * tf, :], preferred_element_type=F32)
    x2 = x1 + mod[5:6] * acc
    if final:
        x2 = _rms(x2, fg_ref[...])
    return x2


def _attn_out_kernel(x_ref, o_ref, mod_ref, ng2_ref, wo_ref, wgu_ref, wd_ref, fg_ref, out_ref,
                     *, n_ff_chunks, final):
    mod = mod_ref[0]
    m = jnp.dot(o_ref[...], wo_ref[...], preferred_element_type=F32)
    x1 = x_ref[...] + mod[2:3] * m
    out_ref[...] = _ffn_tail(x1, mod, ng2_ref[...], wgu_ref, wd_ref, fg_ref, n_ff_chunks, final)


def _ssm_out_kernel(x_ref, yt_ref, mod_ref, ng2_ref, wglu_ref, wgu_ref, wd_ref, fg_ref, out_ref,
                    *, n_ff_chunks, final):
    mod = mod_ref[0]
    gy = jax.nn.gelu(yt_ref[...]).astype(BF16)
    m = lax.dot_general(gy, wglu_ref[...], _TN, preferred_element_type=F32)
    d = m.shape[1] // 2
    r = m[:, :d] * jax.nn.sigmoid(m[:, d:])
    x1 = x_ref[...] + mod[2:3] * r
    out_ref[...] = _ffn_tail(x1, mod, ng2_ref[...], wgu_ref, wd_ref, fg_ref, n_ff_chunks, final)


def _ff_chunks(d_ff):
    return 2 if d_ff % (2 * LANES) == 0 else 1


def _mixer_out_call(kern, name, x, mix, mix_spec, modv, ng2, w_mix, w_gu, w_down, fg, *, seq_len, tm, final):
    n, d = x.shape
    per_seq = seq_len // tm
    tok = lambda i: (i, 0)
    const = lambda i: (0, 0)
    return pl.pallas_call(
        functools.partial(kern, n_ff_chunks=_ff_chunks(w_down.shape[0]), final=final),
        grid=(n // tm,),
        in_specs=[pl.BlockSpec((tm, d), tok),
                  mix_spec,
                  pl.BlockSpec((1, 8, d), lambda i: (i // per_seq, 0, 0)),
                  pl.BlockSpec((1, d), const),
                  pl.BlockSpec(w_mix.shape, const),
                  pl.BlockSpec(w_gu.shape, const),
                  pl.BlockSpec(w_down.shape, const),
                  pl.BlockSpec((1, d), const)],
        out_specs=pl.BlockSpec((tm, d), tok),
        out_shape=jax.ShapeDtypeStruct((n, d), F32),
        compiler_params=_params(1),
        name=name,
    )(x, mix, modv, ng2, w_mix, w_gu, w_down, fg)


def _ssm_in_kernel(x_ref, mod_ref, ng_ref, wt_ref, ut_ref):
    mod = mod_ref[0]
    h = _rms_mod(x_ref[...], ng_ref[...], mod[1:2], mod[0:1]).astype(BF16)
    ut_ref[...] = lax.dot_general(wt_ref[...], h, _NT, preferred_element_type=F32)


def _ssm_in_call(x, modv, ng, w_in_t, *, seq_len, tm):
    n, d = x.shape
    w = w_in_t.shape[0]
    per_seq = seq_len // tm
    return pl.pallas_call(
        _ssm_in_kernel,
        grid=(n // tm,),
        in_specs=[pl.BlockSpec((tm, d), lambda i: (i, 0)),
                  pl.BlockSpec((1, 8, d), lambda i: (i // per_seq, 0, 0)),
                  pl.BlockSpec((1, d), lambda i: (0, 0)),
                  pl.BlockSpec(w_in_t.shape, lambda i: (0, 0))],
        out_specs=pl.BlockSpec((w, tm), lambda i: (0, i)),
        out_shape=jax.ShapeDtypeStruct((w, n), F32),
        compiler_params=_params(1),
        name="ssm_in",
    )(x, modv, ng, w_in_t)


def _cmul(a, b):
    return a[0] * b[0] - a[1] * b[1], a[0] * b[1] + a[1] * b[0]


def _s5_prep_kernel(are_ref, aim_ref, ldt_ref, btre_ref, btim_ref, cre_ref, cim_ref, ctre_ref, ctim_ref,
                    kfull_ref, p_ref, q_ref, ach_ref, *, n_scan):
    are = are_ref[0]
    aim = aim_ref[0]
    dt = jnp.exp(ldt_ref[0])
    mag = jnp.exp(dt * are)
    ab = (mag * jnp.cos(dt * aim), mag * jnp.sin(dt * aim))
    den = are * are + aim * aim
    nr = ab[0] - 1.0
    f = ((nr * are + ab[1] * aim) / den, (ab[1] * are - nr * aim) / den)
    bbt = _cmul(f, (btre_ref[0], btim_ref[0]))
    cre = cre_ref[0]
    cim = cim_ref[0]

    n_blk = CHUNK // 8
    pw = [(jnp.ones_like(are), jnp.zeros_like(are)), ab]
    for _ in range(7):
        pw.append(_cmul(pw[-1], ab))
    hi = [pw[0]]
    for _ in range(n_blk):
        hi.append(_cmul(hi[-1], pw[8]))

    def table(lo_exps, rev):
        lo = (jnp.concatenate([pw[e][0] for e in lo_exps], axis=0),
              jnp.concatenate([pw[e][1] for e in lo_exps], axis=0))
        blocks = [_cmul(hi[n_blk - 1 - i] if rev else hi[i], lo) for i in range(n_blk)]
        return (jnp.concatenate([b[0] for b in blocks], axis=0),
                jnp.concatenate([b[1] for b in blocks], axis=0))

    t_up = table(range(0, 8), False)
    t_up1 = table(range(1, 9), False)
    t_dn = table(range(7, -1, -1), True)
    t_dn1 = table(range(8, 0, -1), True)

    lane = lax.broadcasted_iota(jnp.int32, (CHUNK, 2 * STATE_DIM), 1)
    fwd_lane = lane < STATE_DIM
    row = lax.broadcasted_iota(jnp.int32, (2 * STATE_DIM, CHUNK), 0)
    fwd_row = row < STATE_DIM
    col = lax.broadcasted_iota(jnp.int32, (2 * STATE_DIM, CHUNK), 1)

    rowtab = (jnp.where(fwd_lane, t_dn[0], t_up[0]), jnp.where(fwd_lane, t_dn[1], t_up[1]))
    for ci in range(GROUP_CH):
        xr, xi = _cmul(rowtab, (bbt[0][ci:ci + 1], bbt[1][ci:ci + 1]))
        p_ref[0, ci * CHUNK:(ci + 1) * CHUNK, 0:2 * STATE_DIM] = xr.astype(BF16)
        p_ref[0, ci * CHUNK:(ci + 1) * CHUNK, 2 * STATE_DIM:] = xi.astype(BF16)

    up_t = (t_up[0].T, t_up[1].T)
    up1_t = (t_up1[0].T, t_up1[1].T)
    dn1_t = (t_dn1[0].T, t_dn1[1].T)

    zero = jnp.zeros_like(up_t[0])
    e0 = jnp.where(col == 0, 1.0, 0.0).astype(F32)
    tk_re = jnp.concatenate([jnp.where(fwd_row, zero, dn1_t[0]), jnp.where(fwd_row, up_t[0], e0)], axis=1)
    tk_im = jnp.concatenate([jnp.where(fwd_row, zero, dn1_t[1]), jnp.where(fwd_row, up_t[1], zero)], axis=1)
    w_re = jnp.concatenate([cre * bbt[0][ci:ci + 1] - cim * bbt[1][ci:ci + 1] for ci in range(GROUP_CH)], axis=0)
    w_im = jnp.concatenate([cre * bbt[1][ci:ci + 1] + cim * bbt[0][ci:ci + 1] for ci in range(GROUP_CH)], axis=0)
    kfull_ref[0] = (jnp.dot(w_re, tk_re, preferred_element_type=F32, precision=HIGHEST)
                    - jnp.dot(w_im, tk_im, preferred_element_type=F32, precision=HIGHEST))

    lt = (jnp.where(fwd_row, up1_t[0], dn1_t[0]), jnp.where(fwd_row, up1_t[1], dn1_t[1]))
    ctre = ctre_ref[0]
    ctim = ctim_ref[0]
    for co in range(GROUP_CH):
        er, ei = _cmul((ctre[:, co:co + 1], ctim[:, co:co + 1]), lt)
        q_ref[0, 0:2 * STATE_DIM, co * CHUNK:(co + 1) * CHUNK] = er.astype(BF16)
        q_ref[0, 2 * STATE_DIM:, co * CHUNK:(co + 1) * CHUNK] = (-ei).astype(BF16)

    rows = []
    cur = hi[n_blk]
    for _ in range(n_scan):
        rows.append(jnp.concatenate([cur[0], cur[1]], axis=1))
        cur = _cmul(cur, cur)
    rows += [jnp.zeros_like(rows[0])] * (8 - n_scan)
    ach_ref[0] = jnp.concatenate(rows, axis=0)


def _s5_prep_call(merged, *, n_scan):
    g = merged[0].shape[0]
    p2 = 2 * STATE_DIM
    gk = GROUP_CH * CHUNK
    row_spec = pl.BlockSpec((1, 1, p2), lambda i: (i, 0, 0))
    gc_spec = pl.BlockSpec((1, GROUP_CH, p2), lambda i: (i, 0, 0))
    t_spec = pl.BlockSpec((1, p2, GROUP_CH), lambda i: (i, 0, 0))
    return pl.pallas_call(
        functools.partial(_s5_prep_kernel, n_scan=n_scan),
        grid=(g,),
        in_specs=[row_spec, row_spec, row_spec, gc_spec, gc_spec, gc_spec, gc_spec, t_spec, t_spec],
        out_specs=[pl.BlockSpec((1, GROUP_CH * GROUP_CH, 2 * CHUNK), lambda i: (i, 0, 0)),
                   pl.BlockSpec((1, gk, 2 * p2), lambda i: (i, 0, 0)),
                   pl.BlockSpec((1, 2 * p2, gk), lambda i: (i, 0, 0)),
                   pl.BlockSpec((1, 8, 2 * p2), lambda i: (i, 0, 0))],
        out_shape=[jax.ShapeDtypeStruct((g, GROUP_CH * GROUP_CH, 2 * CHUNK), F32),
                   jax.ShapeDtypeStruct((g, gk, 2 * p2), BF16),
                   jax.ShapeDtypeStruct((g, 2 * p2, gk), BF16),
                   jax.ShapeDtypeStruct((g, 8, 2 * p2), F32)],
        compiler_params=_params(1),
        name="s5_prep",
    )(*merged)


def _s5_kernel(*refs, n_streams, n_scan):
    ut_refs = refs[:n_streams]
    kfull_ref, p_ref, q_ref, ach_ref, d_ref, pos_ref, rem_ref = refs[n_streams:n_streams + 7]
    yt_refs = refs[n_streams + 7:2 * n_streams + 7]
    m_scr = refs[2 * n_streams + 7]

    def build(ci, carry):
        for co in range(GROUP_CH):
            krow = kfull_ref[0, pl.ds(ci * GROUP_CH + co, 1), :]
            spread = pltpu.roll(jnp.broadcast_to(krow, (CHUNK, 2 * CHUNK)), 0, 1, stride=1, stride_axis=0)
            m_scr[pl.ds(pl.multiple_of(ci * CHUNK, CHUNK), CHUNK), co * CHUNK:(co + 1) * CHUNK] = (
                spread[:, CHUNK:].astype(BF16))
        return carry

    lax.fori_loop(0, GROUP_CH, build, 0)

    def chan(ci):
        parts = [r[ci] for r in ut_refs]
        return parts[0] if n_streams == 1 else jnp.concatenate(parts, axis=0)

    u2 = jnp.concatenate([chan(ci).astype(BF16) for ci in range(GROUP_CH)], axis=1)
    nc = u2.shape[0]

    z = jnp.dot(u2, p_ref[0], preferred_element_type=F32)
    p2 = 2 * STATE_DIM
    s_re = z[:, :p2]
    s_im = z[:, p2:]
    fwd = lax.broadcasted_iota(jnp.int32, (nc, p2), 1) < STATE_DIM
    room = jnp.where(fwd, pos_ref[...], rem_ref[...])

    def shifted(x, dist):
        return jnp.where(fwd, pltpu.roll(x, dist, 0), pltpu.roll(x, nc - dist, 0))

    for step in range(n_scan):
        dist = 1 << step
        mul = (ach_ref[0, step:step + 1, :p2], ach_ref[0, step:step + 1, p2:])
        add = _cmul(mul, (shifted(s_re, dist), shifted(s_im, dist)))
        ok = room >= dist
        s_re = s_re + jnp.where(ok, add[0], 0.0)
        s_im = s_im + jnp.where(ok, add[1], 0.0)
    ok = room >= 1
    sx = jnp.concatenate([jnp.where(ok, shifted(s_re, 1), 0.0), jnp.where(ok, shifted(s_im, 1), 0.0)],
                         axis=1).astype(BF16)

    y = (jnp.dot(u2, m_scr[...], preferred_element_type=F32)
         + jnp.dot(sx, q_ref[0], preferred_element_type=F32))
    for co in range(GROUP_CH):
        yc = y[:, co * CHUNK:(co + 1) * CHUNK] + d_ref[0, co] * chan(co)
        start = 0
        for r in yt_refs:
            rows = r.shape[1]
            r[co] = yc[start:start + rows]
            start += rows


def _s5_call(uts, tables, d_lanes, pos, rem, *, n_scan):
    kfull, p_tab, q_tab, ach = tables
    g = kfull.shape[0]
    n_streams = len(uts)
    gk = GROUP_CH * CHUNK
    views = [u.reshape(u.shape[0], u.shape[1] // CHUNK, CHUNK) for u in uts]
    nc = pos.shape[0]
    ut_specs = [pl.BlockSpec((GROUP_CH, v.shape[1], CHUNK), lambda i: (i, 0, 0)) for v in views]
    per_g = lambda i: (i, 0, 0)
    outs = pl.pallas_call(
        functools.partial(_s5_kernel, n_streams=n_streams, n_scan=n_scan),
        grid=(g,),
        in_specs=ut_specs + [pl.BlockSpec((1,) + kfull.shape[1:], per_g),
                             pl.BlockSpec((1,) + p_tab.shape[1:], per_g),
                             pl.BlockSpec((1,) + q_tab.shape[1:], per_g),
                             pl.BlockSpec((1,) + ach.shape[1:], per_g),
                             pl.BlockSpec((1, GROUP_CH, 1, LANES), lambda i: (i, 0, 0, 0)),
                             pl.BlockSpec((nc, 1), lambda i: (0, 0)),
                             pl.BlockSpec((nc, 1), lambda i: (0, 0))],
        out_specs=ut_specs,
        out_shape=[jax.ShapeDtypeStruct(v.shape, F32) for v in views],
        scratch_shapes=[pltpu.VMEM((gk, gk), BF16)],
        compiler_params=_params(1),
        name="s5_core",
    )(*views, kfull, p_tab, q_tab, ach, d_lanes, pos, rem)
    return [o.reshape(u.shape) for o, u in zip(outs, uts)]


def _rope_tables(max_len):
    rows = max_len // GRID_W
    row = jnp.repeat(jnp.arange(rows), GRID_W).astype(F32)
    col = jnp.tile(jnp.arange(GRID_W), rows).astype(F32)
    n_freq = HEAD_DIM // 4
    inv = ROPE_THETA ** (-jnp.arange(n_freq, dtype=F32) / n_freq)
    ang_r = row[:, None] * inv
    ang_c = col[:, None] * inv
    zero = jnp.zeros_like(ang_r)
    cos = jnp.concatenate([jnp.cos(ang_r), jnp.cos(ang_r), jnp.cos(ang_c), jnp.cos(ang_c)], axis=1)
    sin_a = jnp.concatenate([-jnp.sin(ang_r), zero, -jnp.sin(ang_c), zero], axis=1)
    sin_b = jnp.concatenate([zero, jnp.sin(ang_r), zero, jnp.sin(ang_c)], axis=1)
    return cos, sin_a, sin_b


def _merge_dirs(x):
    return jnp.concatenate([x[0], x[1]], axis=-1)


def _s5_merged_params(a_re, a_im, log_dt, b_re, b_im, c_re, c_im):
    g = a_re.shape[1]
    ldt = jnp.broadcast_to(log_dt[:, :, None], a_re.shape)
    row = lambda x: _merge_dirs(x).reshape(g, 1, 2 * STATE_DIM)
    bt = lambda x: _merge_dirs(jnp.swapaxes(x, -1, -2))
    ct = lambda x: jnp.concatenate([jnp.swapaxes(x[0], -1, -2), jnp.swapaxes(x[1], -1, -2)], axis=1)
    return (row(a_re), row(a_im), row(ldt), bt(b_re), bt(b_im), _merge_dirs(c_re), _merge_dirs(c_im),
            ct(c_re), ct(c_im))


def _tile(seq_len, want):
    t = min(want, seq_len)
    assert seq_len % t == 0
    return t


def kernel(x_prompt, x_sample, c_prompt, c_sample, norm_gain, w_mod, b_mod, attn_w_qkv, attn_q_gain, attn_k_gain, attn_w_o, ssm_w_in, ssm_a_re, ssm_a_im, ssm_log_dt, ssm_b_re, ssm_b_im, ssm_c_re, ssm_c_im, ssm_d, ssm_w_glu, ffn_w_gu, ffn_w_down, final_gain):
    depth = w_mod.shape[0]
    d = x_prompt.shape[-1]
    streams = [(x_prompt.shape[0], x_prompt.shape[1]), (x_sample.shape[0], x_sample.shape[1])]
    xs = [x_prompt.reshape(-1, d), x_sample.reshape(-1, d)]
    for _, seq_len in streams:
        assert seq_len % CHUNK == 0 and seq_len % GRID_W == 0

    c_all = jnp.concatenate([c_prompt, c_sample], axis=0)
    n_c = c_all.shape[0]
    c_pad = jnp.pad(c_all, ((0, -n_c % 8), (0, 0)))
    mods = _mod_call(c_pad, w_mod, b_mod)

    def modv(layer, first, batch):
        m = mods[layer, first:first + batch].reshape(batch, 6, d)
        return jnp.pad(m, ((0, 0), (0, 2), (0, 0)))
    firsts = [0, streams[0][0]]

    rope = _rope_tables(max(s for _, s in streams))
    fg = final_gain.reshape(1, d)

    per_seq = [s // CHUNK for b, s in streams for _ in range(b)]
    pos = np.concatenate([np.arange(n) for n in per_seq]).astype(np.int32).reshape(-1, 1)
    rem = np.concatenate([np.arange(n)[::-1] for n in per_seq]).astype(np.int32).reshape(-1, 1)
    n_scan = max(1, math.ceil(math.log2(max(per_seq))))
    assert n_scan <= 8

    for layer in range(depth):
        j = layer // 2
        final = layer == depth - 1
        ng1 = norm_gain[layer, 0].reshape(1, d)
        ng2 = norm_gain[layer, 1].reshape(1, d)
        w_gu = ffn_w_gu[layer].astype(BF16)
        w_down = ffn_w_down[layer].astype(BF16)
        mvs = [modv(layer, firsts[s], streams[s][0]) for s in range(2)]
        if layer % 2 == 0:
            w_qkv = attn_w_qkv[j].astype(BF16)
            w_o = attn_w_o[j].astype(BF16)
            qg = attn_q_gain[j].reshape(1, HEAD_DIM)
            kg = attn_k_gain[j].reshape(1, HEAD_DIM)
            for s, (batch, seq_len) in enumerate(streams):
                tm = _tile(seq_len, 512)
                q, k, v = _attn_in_call(xs[s], mvs[s], ng1, w_qkv, qg, kg, rope, seq_len=seq_len, tm=tm)
                o = _flash_call(q, k, v, batch=batch, seq_len=seq_len,
                                tq=_tile(seq_len, 256), tk=_tile(seq_len, 512))
                xs[s] = _mixer_out_call(_attn_out_kernel, "attn_out_ffn", xs[s], o,
                                        pl.BlockSpec((tm, d), lambda i: (i, 0)),
                                        mvs[s], ng2, w_o, w_gu, w_down, fg,
                                        seq_len=seq_len, tm=tm, final=final)
        else:
            w_in_t = ssm_w_in[j].T.astype(BF16)
            w_glu = ssm_w_glu[j].astype(BF16)
            tables = _s5_prep_call(_s5_merged_params(ssm_a_re[j], ssm_a_im[j], ssm_log_dt[j], ssm_b_re[j],
                                                     ssm_b_im[j], ssm_c_re[j], ssm_c_im[j]), n_scan=n_scan)
            n_groups = ssm_a_re.shape[2]
            d_lanes = jnp.broadcast_to(ssm_d[j].reshape(n_groups, GROUP_CH, 1, 1),
                                       (n_groups, GROUP_CH, 1, LANES))
            uts = [_ssm_in_call(xs[s], mvs[s], ng1, w_in_t, seq_len=streams[s][1],
                                tm=_tile(streams[s][1], 512)) for s in range(2)]
            yts = _s5_call(uts, tables, d_lanes, jnp.asarray(pos), jnp.asarray(rem), n_scan=n_scan)
            for s, (batch, seq_len) in enumerate(streams):
                tm = _tile(seq_len, 512)
                xs[s] = _mixer_out_call(_ssm_out_kernel, "ssm_out_ffn", xs[s], yts[s],
                                        pl.BlockSpec((yts[s].shape[0], tm), lambda i: (0, i)),
                                        mvs[s], ng2, w_glu, w_gu, w_down, fg,
                                        seq_len=seq_len, tm=tm, final=final)
    return (xs[0].reshape(x_prompt.shape), xs[1].reshape(x_sample.shape))
```

```python
import functools
import math

import numpy as np
import jax
import jax.numpy as jnp
from jax import lax
from jax.experimental import pallas as pl
from jax.experimental.pallas import tpu as pltpu

F32 = jnp.float32
BF16 = jnp.bfloat16
HIGHEST = lax.Precision.HIGHEST

HEAD_DIM = 128
GRID_W = 64
ROPE_THETA = 10000.0
EPS = 1e-6
LANES = 128
CHUNK = LANES
GROUP_CH = 16
STATE_DIM = 64
VMEM_LIMIT = 56 * 1024 * 1024

_NT = (((1,), (1,)), ((), ()))
_TN = (((0,), (0,)), ((), ()))


def _params(n_axes, vmem=VMEM_LIMIT):
    return pltpu.CompilerParams(dimension_semantics=("parallel",) * n_axes,
                                vmem_limit_bytes=vmem)


def _rms(x, gain):
    ms = jnp.mean(x * x, axis=-1, keepdims=True)
    return x * lax.rsqrt(ms + EPS) * gain


def _rms_mod(x, gain, scale, shift):
    return _rms(x, gain) * (1.0 + scale) + shift


def _mod_kernel(c_ref, w_ref, b_ref, o_ref):
    c = c_ref[...]
    ca = c * jax.nn.sigmoid(c)
    o_ref[0] = jnp.dot(ca, w_ref[0], preferred_element_type=F32, precision=HIGHEST) + b_ref[0]


def _mod_call(c_pad, w_mod, b_mod):
    depth, d, d6 = w_mod.shape
    rows = c_pad.shape[0]
    return pl.pallas_call(
        _mod_kernel,
        grid=(depth, d6 // d),
        in_specs=[pl.BlockSpec((rows, d), lambda i, j: (0, 0)),
                  pl.BlockSpec((1, d, d), lambda i, j: (i, 0, j)),
                  pl.BlockSpec((1, 1, d), lambda i, j: (i, 0, j))],
        out_specs=pl.BlockSpec((1, rows, d), lambda i, j: (i, 0, j)),
        out_shape=jax.ShapeDtypeStruct((depth, rows, d6), F32),
        compiler_params=_params(2),
        name="adaln_mod",
    )(c_pad, w_mod, b_mod.reshape(depth, 1, d6))


def _attn_in_kernel(x_ref, mod_ref, ng_ref, w_ref, qg_ref, kg_ref, cos_ref, sin_ref,
                    q_ref, k_ref, v_ref, *, n_q, n_kv):
    mod = mod_ref[0]
    h = _rms_mod(x_ref[...], ng_ref[...], mod[1:2], mod[0:1]).astype(BF16)
    cos = cos_ref[...]
    sin = sin_ref[...]

    pairs = {}

    def head(hd):
        pr = hd // 2
        if pr not in pairs:
            pairs[pr] = jnp.dot(h, w_ref[:, pr * 2 * HEAD_DIM:(pr + 1) * 2 * HEAD_DIM],
                                preferred_element_type=F32)
        return pairs[pr][:, (hd % 2) * HEAD_DIM:(hd % 2 + 1) * HEAD_DIM]

    def norm_rope(t, gain):
        t = _rms(t, gain)
        return t * cos + pltpu.roll(t, HEAD_DIM // 2, 1) * sin

    q_gain = qg_ref[...] * (HEAD_DIM ** -0.5 * math.log2(math.e))
    for hd in range(n_q):
        q_ref[:, hd * HEAD_DIM:(hd + 1) * HEAD_DIM] = norm_rope(head(hd), q_gain).astype(BF16)
    for hd in range(n_kv):
        k_ref[:, hd * HEAD_DIM:(hd + 1) * HEAD_DIM] = norm_rope(head(n_q + hd), kg_ref[...]).astype(BF16)
    ones = jnp.ones((h.shape[0], HEAD_DIM), BF16)
    for hd in range(n_kv):
        v_ref[:, 2 * hd * HEAD_DIM:(2 * hd + 1) * HEAD_DIM] = head(n_q + n_kv + hd).astype(BF16)
        v_ref[:, (2 * hd + 1) * HEAD_DIM:(2 * hd + 2) * HEAD_DIM] = ones


def _attn_in_call(x, modv, ng, w_qkv, qg, kg, rope, *, seq_len, tm):
    n, d = x.shape
    n_q = d // HEAD_DIM
    n_kv = (w_qkv.shape[1] // HEAD_DIM - n_q) // 2
    per_seq = seq_len // tm
    tok = lambda i: (i, 0)
    const = lambda i: (0, 0)
    pos = lambda i: (i % per_seq, 0)
    kv_w = n_kv * HEAD_DIM
    return pl.pallas_call(
        functools.partial(_attn_in_kernel, n_q=n_q, n_kv=n_kv),
        grid=(n // tm,),
        in_specs=[pl.BlockSpec((tm, d), tok),
                  pl.BlockSpec((1, 8, d), lambda i: (i // per_seq, 0, 0)),
                  pl.BlockSpec((1, d), const),
                  pl.BlockSpec(w_qkv.shape, const),
                  pl.BlockSpec((1, HEAD_DIM), const),
                  pl.BlockSpec((1, HEAD_DIM), const),
                  pl.BlockSpec((tm, HEAD_DIM), pos),
                  pl.BlockSpec((tm, HEAD_DIM), pos)],
        out_specs=[pl.BlockSpec((tm, d), tok),
                   pl.BlockSpec((tm, kv_w), tok),
                   pl.BlockSpec((tm, 2 * kv_w), tok)],
        out_shape=[jax.ShapeDtypeStruct((n, d), BF16),
                   jax.ShapeDtypeStruct((n, kv_w), BF16),
                   jax.ShapeDtypeStruct((n, 2 * kv_w), BF16)],
        compiler_params=_params(1),
        name="attn_in",
    )(x, modv, ng, w_qkv, qg, kg, *rope)


def _flash_kernel(q_ref, k_ref, v_ref, o_ref, q_scr, s0, s1, p0, p1, a0, a1, m_scr, acc_scr, *, tk, group):
    tq = q_ref.shape[0]
    n_chunks = k_ref.shape[0] // tk
    for g in range(group):
        q_scr[g * tq:(g + 1) * tq, :] = q_ref[:, g * HEAD_DIM:(g + 1) * HEAD_DIM]
    m_scr[...] = jnp.full(m_scr.shape, -jnp.inf, F32)
    acc_scr[...] = jnp.zeros(acc_scr.shape, F32)

    def chunk(ref, j):
        return ref[pl.ds(pl.multiple_of(j * tk, tk), tk), :]

    def scores(j, s_out):
        s_out[...] = lax.dot_general(q_scr[...], chunk(k_ref, j), _NT, preferred_element_type=F32)

    def softmax(s_in, p_out, a_out):
        s = s_in[...]
        m_old = m_scr[...]
        m_new = jnp.maximum(m_old, jnp.max(s, axis=-1, keepdims=True))
        a_out[...] = jnp.exp2(m_old - m_new)
        m_scr[...] = m_new
        p_out[...] = jnp.exp2(s - jnp.concatenate([m_new] * (tk // LANES), axis=1)).astype(BF16)

    def weighted(j, p_in, a_in):
        a = a_in[...]
        pv = jnp.dot(p_in[...], chunk(v_ref, j), preferred_element_type=F32)
        acc_scr[...] = jnp.concatenate([a, a], axis=1) * acc_scr[...] + pv

    p1[...] = jnp.zeros(p1.shape, BF16)
    a1[...] = jnp.ones(a1.shape, F32)
    scores(0, s0)

    def body(i, carry):
        j = 2 * i
        scores(j + 1, s1)
        softmax(s0, p0, a0)
        weighted(jnp.maximum(j - 1, 0), p1, a1)
        scores(jnp.minimum(j + 2, n_chunks - 1), s0)
        softmax(s1, p1, a1)
        weighted(j, p0, a0)
        return carry

    lax.fori_loop(0, n_chunks // 2, body, 0)
    weighted(n_chunks - 1, p1, a1)
    acc = acc_scr[...]
    o = acc[:, :HEAD_DIM] / acc[:, HEAD_DIM:]
    for g in range(group):
        o_ref[:, g * HEAD_DIM:(g + 1) * HEAD_DIM] = o[g * tq:(g + 1) * tq].astype(BF16)


def _flash_call(q, k, v_ext, *, batch, seq_len, tq, tk):
    n, d = q.shape
    n_kv = k.shape[1] // HEAD_DIM
    group = d // HEAD_DIM // n_kv
    gw = group * HEAD_DIM
    per_seq = seq_len // tq
    rows = group * tq
    assert (seq_len // tk) % 2 == 0
    return pl.pallas_call(
        functools.partial(_flash_kernel, tk=tk, group=group),
        grid=(batch, n_kv, per_seq),
        in_specs=[pl.BlockSpec((tq, gw), lambda b, h, i: (b * per_seq + i, h)),
                  pl.BlockSpec((seq_len, HEAD_DIM), lambda b, h, i: (b, h)),
                  pl.BlockSpec((seq_len, 2 * HEAD_DIM), lambda b, h, i: (b, h))],
        out_specs=pl.BlockSpec((tq, gw), lambda b, h, i: (b * per_seq + i, h)),
        out_shape=jax.ShapeDtypeStruct((n, d), BF16),
        scratch_shapes=[pltpu.VMEM((rows, HEAD_DIM), BF16),
                        pltpu.VMEM((rows, tk), F32), pltpu.VMEM((rows, tk), F32),
                        pltpu.VMEM((rows, tk), BF16), pltpu.VMEM((rows, tk), BF16),
                        pltpu.VMEM((rows, LANES), F32), pltpu.VMEM((rows, LANES), F32),
                        pltpu.VMEM((rows, LANES), F32),
                        pltpu.VMEM((rows, 2 * HEAD_DIM), F32)],
        compiler_params=_params(3),
        name="flash_attn",
    )(q, k, v_ext)


def _ffn_tail(x1, mod, ng2, wgu_ref, wd_ref, fg_ref, n_ff_chunks, final):
    d_ff = wd_ref.shape[0]
    tf = d_ff // n_ff_chunks
    h2 = _rms_mod(x1, ng2, mod[4:5], mod[3:4]).astype(BF16)
    acc = jnp.zeros_like(x1)
    for c in range(n_ff_chunks):
        g = jnp.dot(h2, wgu_ref[:, c * tf:(c + 1) * tf], preferred_element_type=F32)
        u = jnp.dot(h2, wgu_ref[:, d_ff + c * tf:d_ff + (c + 1) * tf], preferred_element_type=F32)
        a = (g * jax.nn.sigmoid(g) * u).astype(BF16)
        acc = acc + jnp.dot(a, wd_ref[c * tf:(c + 1) * tf, :], preferred_element_type=F32)
    x2 = x1 + mod[5:6] * acc
    if final:
        x2 = _rms(x2, fg_ref[...])
    return x2


def _attn_out_kernel(x_ref, o_ref, mod_ref, ng2_ref, wo_ref, wgu_ref, wd_ref, fg_ref, out_ref,
                     *, n_ff_chunks, final):
    mod = mod_ref[0]
    m = jnp.dot(o_ref[...], wo_ref[...], preferred_element_type=F32)
    x1 = x_ref[...] + mod[2:3] * m
    out_ref[...] = _ffn_tail(x1, mod, ng2_ref[...], wgu_ref, wd_ref, fg_ref, n_ff_chunks, final)


def _ssm_out_kernel(x_ref, yt_ref, mod_ref, ng2_ref, wglu_ref, wgu_ref, wd_ref, fg_ref, out_ref,
                    *, n_ff_chunks, final):
    mod = mod_ref[0]
    gy = jax.nn.gelu(yt_ref[...]).astype(BF16)
    m = lax.dot_general(gy, wglu_ref[...], _TN, preferred_element_type=F32)
    d = m.shape[1] // 2
    r = m[:, :d] * jax.nn.sigmoid(m[:, d:])
    x1 = x_ref[...] + mod[2:3] * r
    out_ref[...] = _ffn_tail(x1, mod, ng2_ref[...], wgu_ref, wd_ref, fg_ref, n_ff_chunks, final)


def _ff_chunks(d_ff):
    return 2 if d_ff % (2 * LANES) == 0 else 1


def _mixer_out_call(kern, name, x, mix, mix_spec, modv, ng2, w_mix, w_gu, w_down, fg, *, seq_len, tm, final):
    n, d = x.shape
    per_seq = seq_len // tm
    tok = lambda i: (i, 0)
    const = lambda i: (0, 0)
    return pl.pallas_call(
        functools.partial(kern, n_ff_chunks=_ff_chunks(w_down.shape[0]), final=final),
        grid=(n // tm,),
        in_specs=[pl.BlockSpec((tm, d), tok),
                  mix_spec,
                  pl.BlockSpec((1, 8, d), lambda i: (i // per_seq, 0, 0)),
                  pl.BlockSpec((1, d), const),
                  pl.BlockSpec(w_mix.shape, const),
                  pl.BlockSpec(w_gu.shape, const),
                  pl.BlockSpec(w_down.shape, const),
                  pl.BlockSpec((1, d), const)],
        out_specs=pl.BlockSpec((tm, d), tok),
        out_shape=jax.ShapeDtypeStruct((n, d), F32),
        compiler_params=_params(1),
        name=name,
    )(x, mix, modv, ng2, w_mix, w_gu, w_down, fg)


def _ssm_in_kernel(x_ref, mod_ref, ng_ref, wt_ref, ut_ref):
    mod = mod_ref[0]
    h = _rms_mod(x_ref[...], ng_ref[...], mod[1:2], mod[0:1]).astype(BF16)
    ut_ref[...] = lax.dot_general(wt_ref[...], h, _NT, preferred_element_type=F32)


def _ssm_in_call(x, modv, ng, w_in_t, *, seq_len, tm):
    n, d = x.shape
    w = w_in_t.shape[0]
    per_seq = seq_len // tm
    return pl.pallas_call(
        _ssm_in_kernel,
        grid=(n // tm,),
        in_specs=[pl.BlockSpec((tm, d), lambda i: (i, 0)),
                  pl.BlockSpec((1, 8, d), lambda i: (i // per_seq, 0, 0)),
                  pl.BlockSpec((1, d), lambda i: (0, 0)),
                  pl.BlockSpec(w_in_t.shape, lambda i: (0, 0))],
        out_specs=pl.BlockSpec((w, tm), lambda i: (0, i)),
        out_shape=jax.ShapeDtypeStruct((w, n), F32),
        compiler_params=_params(1),
        name="ssm_in",
    )(x, modv, ng, w_in_t)


def _cmul(a, b):
    return a[0] * b[0] - a[1] * b[1], a[0] * b[1] + a[1] * b[0]


def _s5_prep_kernel(are_ref, aim_ref, ldt_ref, btre_ref, btim_ref, cre_ref, cim_ref, ctre_ref, ctim_ref,
                    kfull_ref, p_ref, q_ref, ach_ref, *, n_scan):
    are = are_ref[0]
    aim = aim_ref[0]
    dt = jnp.exp(ldt_ref[0])
    mag = jnp.exp(dt * are)
    ab = (mag * jnp.cos(dt * aim), mag * jnp.sin(dt * aim))
    den = are * are + aim * aim
    nr = ab[0] - 1.0
    f = ((nr * are + ab[1] * aim) / den, (ab[1] * are - nr * aim) / den)
    bbt = _cmul(f, (btre_ref[0], btim_ref[0]))
    cre = cre_ref[0]
    cim = cim_ref[0]

    n_blk = CHUNK // 8
    pw = [(jnp.ones_like(are), jnp.zeros_like(are)), ab]
    for _ in range(7):
        pw.append(_cmul(pw[-1], ab))
    hi = [pw[0]]
    for _ in range(n_blk):
        hi.append(_cmul(hi[-1], pw[8]))

    def table(lo_exps, rev):
        lo = (jnp.concatenate([pw[e][0] for e in lo_exps], axis=0),
              jnp.concatenate([pw[e][1] for e in lo_exps], axis=0))
        blocks = [_cmul(hi[n_blk - 1 - i] if rev else hi[i], lo) for i in range(n_blk)]
        return (jnp.concatenate([b[0] for b in blocks], axis=0),
                jnp.concatenate([b[1] for b in blocks], axis=0))

    t_up = table(range(0, 8), False)
    t_up1 = table(range(1, 9), False)
    t_dn = table(range(7, -1, -1), True)
    t_dn1 = table(range(8, 0, -1), True)

    lane = lax.broadcasted_iota(jnp.int32, (CHUNK, 2 * STATE_DIM), 1)
    fwd_lane = lane < STATE_DIM
    row = lax.broadcasted_iota(jnp.int32, (2 * STATE_DIM, CHUNK), 0)
    fwd_row = row < STATE_DIM
    col = lax.broadcasted_iota(jnp.int32, (2 * STATE_DIM, CHUNK), 1)

    rowtab = (jnp.where(fwd_lane, t_dn[0], t_up[0]), jnp.where(fwd_lane, t_dn[1], t_up[1]))
    for ci in range(GROUP_CH):
        xr, xi = _cmul(rowtab, (bbt[0][ci:ci + 1], bbt[1][ci:ci + 1]))
        p_ref[0, ci * CHUNK:(ci + 1) * CHUNK, 0:2 * STATE_DIM] = xr.astype(BF16)
        p_ref[0, ci * CHUNK:(ci + 1) * CHUNK, 2 * STATE_DIM:] = xi.astype(BF16)

    up_t = (t_up[0].T, t_up[1].T)
    up1_t = (t_up1[0].T, t_up1[1].T)
    dn1_t = (t_dn1[0].T, t_dn1[1].T)

    zero = jnp.zeros_like(up_t[0])
    e0 = jnp.where(col == 0, 1.0, 0.0).astype(F32)
    tk_re = jnp.concatenate([jnp.where(fwd_row, zero, dn1_t[0]), jnp.where(fwd_row, up_t[0], e0)], axis=1)
    tk_im = jnp.concatenate([jnp.where(fwd_row, zero, dn1_t[1]), jnp.where(fwd_row, up_t[1], zero)], axis=1)
    w_re = jnp.concatenate([cre * bbt[0][ci:ci + 1] - cim * bbt[1][ci:ci + 1] for ci in range(GROUP_CH)], axis=0)
    w_im = jnp.concatenate([cre * bbt[1][ci:ci + 1] + cim * bbt[0][ci:ci + 1] for ci in range(GROUP_CH)], axis=0)
    kfull_ref[0] = (jnp.dot(w_re, tk_re, preferred_element_type=F32, precision=HIGHEST)
                    - jnp.dot(w_im, tk_im, preferred_element_type=F32, precision=HIGHEST))

    lt = (jnp.where(fwd_row, up1_t[0], dn1_t[0]), jnp.where(fwd_row, up1_t[1], dn1_t[1]))
    ctre = ctre_ref[0]
    ctim = ctim_ref[0]
    for co in range(GROUP_CH):
        er, ei = _cmul((ctre[:, co:co + 1], ctim[:, co:co + 1]), lt)
        q_ref[0, 0:2 * STATE_DIM, co * CHUNK:(co + 1) * CHUNK] = er.astype(BF16)
        q_ref[0, 2 * STATE_DIM:, co * CHUNK:(co + 1) * CHUNK] = (-ei).astype(BF16)

    rows = []
    cur = hi[n_blk]
    for _ in range(n_scan):
        rows.append(jnp.concatenate([cur[0], cur[1]], axis=1))
        cur = _cmul(cur, cur)
    rows += [jnp.zeros_like(rows[0])] * (8 - n_scan)
    ach_ref[0] = jnp.concatenate(rows, axis=0)


def _s5_prep_call(merged, *, n_scan):
    g = merged[0].shape[0]
    p2 = 2 * STATE_DIM
    gk = GROUP_CH * CHUNK
    row_spec = pl.BlockSpec((1, 1, p2), lambda i: (i, 0, 0))
    gc_spec = pl.BlockSpec((1, GROUP_CH, p2), lambda i: (i, 0, 0))
    t_spec = pl.BlockSpec((1, p2, GROUP_CH), lambda i: (i, 0, 0))
    return pl.pallas_call(
        functools.partial(_s5_prep_kernel, n_scan=n_scan),
        grid=(g,),
        in_specs=[row_spec, row_spec, row_spec, gc_spec, gc_spec, gc_spec, gc_spec, t_spec, t_spec],
        out_specs=[pl.BlockSpec((1, GROUP_CH * GROUP_CH, 2 * CHUNK), lambda i: (i, 0, 0)),
                   pl.BlockSpec((1, gk, 2 * p2), lambda i: (i, 0, 0)),
                   pl.BlockSpec((1, 2 * p2, gk), lambda i: (i, 0, 0)),
                   pl.BlockSpec((1, 8, 2 * p2), lambda i: (i, 0, 0))],
        out_shape=[jax.ShapeDtypeStruct((g, GROUP_CH * GROUP_CH, 2 * CHUNK), F32),
                   jax.ShapeDtypeStruct((g, gk, 2 * p2), BF16),
                   jax.ShapeDtypeStruct((g, 2 * p2, gk), BF16),
                   jax.ShapeDtypeStruct((g, 8, 2 * p2), F32)],
        compiler_params=_params(1),
        name="s5_prep",
    )(*merged)


def _s5_kernel(*refs, n_streams, n_scan):
    ut_refs = refs[:n_streams]
    kfull_ref, p_ref, q_ref, ach_ref, d_ref, pos_ref, rem_ref = refs[n_streams:n_streams + 7]
    yt_refs = refs[n_streams + 7:2 * n_streams + 7]
    m_scr = refs[2 * n_streams + 7]

    def build(ci, carry):
        for co in range(GROUP_CH):
            krow = kfull_ref[0, pl.ds(ci * GROUP_CH + co, 1), :]
            spread = pltpu.roll(jnp.broadcast_to(krow, (CHUNK, 2 * CHUNK)), 0, 1, stride=1, stride_axis=0)
            m_scr[pl.ds(pl.multiple_of(ci * CHUNK, CHUNK), CHUNK), co * CHUNK:(co + 1) * CHUNK] = (
                spread[:, CHUNK:].astype(BF16))
        return carry

    lax.fori_loop(0, GROUP_CH, build, 0)

    def chan(ci):
        parts = [r[ci] for r in ut_refs]
        return parts[0] if n_streams == 1 else jnp.concatenate(parts, axis=0)

    u2 = jnp.concatenate([chan(ci).astype(BF16) for ci in range(GROUP_CH)], axis=1)
    nc = u2.shape[0]

    z = jnp.dot(u2, p_ref[0], preferred_element_type=F32)
    p2 = 2 * STATE_DIM
    s_re = z[:, :p2]
    s_im = z[:, p2:]
    fwd = lax.broadcasted_iota(jnp.int32, (nc, p2), 1) < STATE_DIM
    room = jnp.where(fwd, pos_ref[...], rem_ref[...])

    def shifted(x, dist):
        return jnp.where(fwd, pltpu.roll(x, dist, 0), pltpu.roll(x, nc - dist, 0))

    for step in range(n_scan):
        dist = 1 << step
        mul = (ach_ref[0, step:step + 1, :p2], ach_ref[0, step:step + 1, p2:])
        add = _cmul(mul, (shifted(s_re, dist), shifted(s_im, dist)))
        ok = room >= dist
        s_re = s_re + jnp.where(ok, add[0], 0.0)
        s_im = s_im + jnp.where(ok, add[1], 0.0)
    ok = room >= 1
    sx = jnp.concatenate([jnp.where(ok, shifted(s_re, 1), 0.0), jnp.where(ok, shifted(s_im, 1), 0.0)],
                         axis=1).astype(BF16)

    y = (jnp.dot(u2, m_scr[...], preferred_element_type=F32)
         + jnp.dot(sx, q_ref[0], preferred_element_type=F32))
    for co in range(GROUP_CH):
        yc = y[:, co * CHUNK:(co + 1) * CHUNK] + d_ref[0, co] * chan(co)
        start = 0
        for r in yt_refs:
            rows = r.shape[1]
            r[co] = yc[start:start + rows]
            start += rows


def _s5_call(uts, tables, d_lanes, pos, rem, *, n_scan):
    kfull, p_tab, q_tab, ach = tables
    g = kfull.shape[0]
    n_streams = len(uts)
    gk = GROUP_CH * CHUNK
    views = [u.reshape(u.shape[0], u.shape[1] // CHUNK, CHUNK) for u in uts]
    nc = pos.shape[0]
    ut_specs = [pl.BlockSpec((GROUP_CH, v.shape[1], CHUNK), lambda i: (i, 0, 0)) for v in views]
    per_g = lambda i: (i, 0, 0)
    outs = pl.pallas_call(
        functools.partial(_s5_kernel, n_streams=n_streams, n_scan=n_scan),
        grid=(g,),
        in_specs=ut_specs + [pl.BlockSpec((1,) + kfull.shape[1:], per_g),
                             pl.BlockSpec((1,) + p_tab.shape[1:], per_g),
                             pl.BlockSpec((1,) + q_tab.shape[1:], per_g),
                             pl.BlockSpec((1,) + ach.shape[1:], per_g),
                             pl.BlockSpec((1, GROUP_CH, 1, LANES), lambda i: (i, 0, 0, 0)),
                             pl.BlockSpec((nc, 1), lambda i: (0, 0)),
                             pl.BlockSpec((nc, 1), lambda i: (0, 0))],
        out_specs=ut_specs,
        out_shape=[jax.ShapeDtypeStruct(v.shape, F32) for v in views],
        scratch_shapes=[pltpu.VMEM((gk, gk), BF16)],
        compiler_params=_params(1),
        name="s5_core",
    )(*views, kfull, p_tab, q_tab, ach, d_lanes, pos, rem)
    return [o.reshape(u.shape) for o, u in zip(outs, uts)]


_QUARTER = HEAD_DIM // 4
_ROPE_PERM = np.concatenate([np.arange(0, _QUARTER), np.arange(2 * _QUARTER, 3 * _QUARTER),
                             np.arange(_QUARTER, 2 * _QUARTER), np.arange(3 * _QUARTER, 4 * _QUARTER)])


def _rope_tables(max_len):
    rows = max_len // GRID_W
    row = jnp.repeat(jnp.arange(rows), GRID_W).astype(F32)
    col = jnp.tile(jnp.arange(GRID_W), rows).astype(F32)
    inv = ROPE_THETA ** (-jnp.arange(_QUARTER, dtype=F32) / _QUARTER)
    ang_r = row[:, None] * inv
    ang_c = col[:, None] * inv
    cos = jnp.concatenate([jnp.cos(ang_r), jnp.cos(ang_c), jnp.cos(ang_r), jnp.cos(ang_c)], axis=1)
    sin = jnp.concatenate([-jnp.sin(ang_r), -jnp.sin(ang_c), jnp.sin(ang_r), jnp.sin(ang_c)], axis=1)
    return cos, sin


def _permute_qk_heads(w_qkv, n_qk_heads):
    cols = np.concatenate([h * HEAD_DIM + _ROPE_PERM for h in range(n_qk_heads)]
                          + [np.arange(n_qk_heads * HEAD_DIM, w_qkv.shape[1])])
    return w_qkv[:, cols]


def _merge_dirs(x):
    return jnp.concatenate([x[0], x[1]], axis=-1)


def _s5_merged_params(a_re, a_im, log_dt, b_re, b_im, c_re, c_im):
    g = a_re.shape[1]
    ldt = jnp.broadcast_to(log_dt[:, :, None], a_re.shape)
    row = lambda x: _merge_dirs(x).reshape(g, 1, 2 * STATE_DIM)
    bt = lambda x: _merge_dirs(jnp.swapaxes(x, -1, -2))
    ct = lambda x: jnp.concatenate([jnp.swapaxes(x[0], -1, -2), jnp.swapaxes(x[1], -1, -2)], axis=1)
    return (row(a_re), row(a_im), row(ldt), bt(b_re), bt(b_im), _merge_dirs(c_re), _merge_dirs(c_im),
            ct(c_re), ct(c_im))


def _tile(seq_len, want):
    t = min(want, seq_len)
    assert seq_len % t == 0
    return t


def kernel(x_prompt, x_sample, c_prompt, c_sample, norm_gain, w_mod, b_mod, attn_w_qkv, attn_q_gain, attn_k_gain, attn_w_o, ssm_w_in, ssm_a_re, ssm_a_im, ssm_log_dt, ssm_b_re, ssm_b_im, ssm_c_re, ssm_c_im, ssm_d, ssm_w_glu, ffn_w_gu, ffn_w_down, final_gain):
    depth = w_mod.shape[0]
    d = x_prompt.shape[-1]
    streams = [(x_prompt.shape[0], x_prompt.shape[1]), (x_sample.shape[0], x_sample.shape[1])]
    xs = [x_prompt.reshape(-1, d), x_sample.reshape(-1, d)]
    for _, seq_len in streams:
        assert seq_len % CHUNK == 0 and seq_len % GRID_W == 0

    c_all = jnp.concatenate([c_prompt, c_sample], axis=0)
    n_c = c_all.shape[0]
    c_pad = jnp.pad(c_all, ((0, -n_c % 8), (0, 0)))
    mods = _mod_call(c_pad, w_mod, b_mod)

    def modv(layer, first, batch):
        m = mods[layer, first:first + batch].reshape(batch, 6, d)
        return jnp.pad(m, ((0, 0), (0, 2), (0, 0)))
    firsts = [0, streams[0][0]]

    rope = _rope_tables(max(s for _, s in streams))
    fg = final_gain.reshape(1, d)

    per_seq = [s // CHUNK for b, s in streams for _ in range(b)]
    pos = np.concatenate([np.arange(n) for n in per_seq]).astype(np.int32).reshape(-1, 1)
    rem = np.concatenate([np.arange(n)[::-1] for n in per_seq]).astype(np.int32).reshape(-1, 1)
    n_scan = max(1, math.ceil(math.log2(max(per_seq))))
    assert n_scan <= 8

    for layer in range(depth):
        j = layer // 2
        final = layer == depth - 1
        ng1 = norm_gain[layer, 0].reshape(1, d)
        ng2 = norm_gain[layer, 1].reshape(1, d)
        w_gu = ffn_w_gu[layer].astype(BF16)
        w_down = ffn_w_down[layer].astype(BF16)
        mvs = [modv(layer, firsts[s], streams[s][0]) for s in range(2)]
        if layer % 2 == 0:
            n_qk_heads = d // HEAD_DIM + (attn_w_qkv.shape[2] - d) // (2 * HEAD_DIM)
            w_qkv = _permute_qk_heads(attn_w_qkv[j], n_qk_heads).astype(BF16)
            w_o = attn_w_o[j].astype(BF16)
            qg = attn_q_gain[j][_ROPE_PERM].reshape(1, HEAD_DIM)
            kg = attn_k_gain[j][_ROPE_PERM].reshape(1, HEAD_DIM)
            for s, (batch, seq_len) in enumerate(streams):
                tm = _tile(seq_len, 512)
                q, k, v = _attn_in_call(xs[s], mvs[s], ng1, w_qkv, qg, kg, rope, seq_len=seq_len, tm=tm)
                o = _flash_call(q, k, v, batch=batch, seq_len=seq_len,
                                tq=_tile(seq_len, 512), tk=_tile(seq_len // 2, 512))
                xs[s] = _mixer_out_call(_attn_out_kernel, "attn_out_ffn", xs[s], o,
                                        pl.BlockSpec((tm, d), lambda i: (i, 0)),
                                        mvs[s], ng2, w_o, w_gu, w_down, fg,
                                        seq_len=seq_len, tm=tm, final=final)
        else:
            w_in_t = ssm_w_in[j].T.astype(BF16)
            w_glu = ssm_w_glu[j].astype(BF16)
            tables = _s5_prep_call(_s5_merged_params(ssm_a_re[j], ssm_a_im[j], ssm_log_dt[j], ssm_b_re[j],
                                                     ssm_b_im[j], ssm_c_re[j], ssm_c_im[j]), n_scan=n_scan)
            n_groups = ssm_a_re.shape[2]
            d_lanes = jnp.broadcast_to(ssm_d[j].reshape(n_groups, GROUP_CH, 1, 1),
                                       (n_groups, GROUP_CH, 1, LANES))
            uts = [_ssm_in_call(xs[s], mvs[s], ng1, w_in_t, seq_len=streams[s][1],
                                tm=_tile(streams[s][1], 512)) for s in range(2)]
            yts = _s5_call(uts, tables, d_lanes, jnp.asarray(pos), jnp.asarray(rem), n_scan=n_scan)
            for s, (batch, seq_len) in enumerate(streams):
                tm = _tile(seq_len, 512)
                xs[s] = _mixer_out_call(_ssm_out_kernel, "ssm_out_ffn", xs[s], yts[s],
                                        pl.BlockSpec((yts[s].shape[0], tm), lambda i: (0, i)),
                                        mvs[s], ng2, w_glu, w_gu, w_down, fg,
                                        seq_len=seq_len, tm=tm, final=final)
    return (xs[0].reshape(x_prompt.shape), xs[1].reshape(x_sample.shape))
```
